```python
import math
import jax, jax.numpy as jnp
from jax import lax
import numpy as np

D_MODEL = 1024
BATCH = 8
SEQ = 4096
DEPTH = 2

N_META = 16
D_MIX = D_MODEL
CONV_DIM = D_MIX // 4
ATT_HEAD_DIM = 64
ATT_DIM = D_MIX // 2
ATT_HEADS = ATT_DIM // ATT_HEAD_DIM
SSM_DIM = D_MIX - CONV_DIM - ATT_DIM
SSM_GROUP = 16
SSM_GROUPS = SSM_DIM // SSM_GROUP
SSM_STATE = 64
CONV_WIDTH = 31
BLOCK_Q = 128
N_EXPERT_GROUPS = 4
EXPERTS_PER_GROUP = 4
N_EXPERTS = N_EXPERT_GROUPS * EXPERTS_PER_GROUP
TOP_K_INNER = 2
D_FF_EXPERT = D_MODEL // 4
EPS = 1e-6
NEG_BIG = -1e30

IN_WIDTHS = (CONV_DIM, CONV_DIM, ATT_DIM, ATT_DIM, ATT_DIM, ATT_HEADS, SSM_DIM)
IN_SPLITS = tuple(int(v) for v in np.cumsum(IN_WIDTHS)[:-1])
D_IN = int(sum(IN_WIDTHS))

kernel_name = "hymba_conformer_fox_s5_hiermoe"


def rms_norm(x, g):
    xf = x.astype(jnp.float32)
    y = xf * lax.rsqrt(jnp.mean(xf * xf, axis=-1, keepdims=True) + EPS)
    return (y * g.astype(jnp.float32)).astype(x.dtype)


def conformer_conv(a, gate, w_dw, b_dw, ln_g, ln_b):
    u = a * jax.nn.sigmoid(gate)
    u_pad = jnp.pad(u, ((0, 0), (CONV_WIDTH - 1, 0), (0, 0)))
    y = lax.conv_general_dilated(
        u_pad, w_dw[:, None, :].astype(u.dtype), window_strides=(1,), padding="VALID",
        dimension_numbers=("NWC", "WIO", "NWC"), feature_group_count=CONV_DIM)
    y = y.astype(jnp.float32) + b_dw.astype(jnp.float32)
    mu = jnp.mean(y, axis=-1, keepdims=True)
    var = jnp.mean(jnp.square(y - mu), axis=-1, keepdims=True)
    y = (y - mu) * lax.rsqrt(var + EPS) * ln_g.astype(jnp.float32) + ln_b.astype(jnp.float32)
    return jax.nn.silu(y).astype(a.dtype)


def forgetting_attention(q, k, v, f_logit):
    bsz, t_len = q.shape[0], q.shape[1]
    pad = BLOCK_Q - N_META
    l_len = t_len + pad
    n_blocks = l_len // BLOCK_Q
    cum = jnp.cumsum(jax.nn.log_sigmoid(f_logit.astype(jnp.float32)), axis=1)
    padt = lambda z: jnp.pad(z, ((0, 0), (pad, 0)) + ((0, 0),) * (z.ndim - 2))
    q, k, v, cum = padt(q), padt(k), padt(v), padt(cum)
    cum_k = cum.transpose(0, 2, 1)
    qb = q.reshape(bsz, n_blocks, BLOCK_Q, ATT_HEADS, ATT_HEAD_DIM).transpose(1, 0, 2, 3, 4)
    cqb = cum_k.reshape(bsz, ATT_HEADS, n_blocks, BLOCK_Q).transpose(2, 0, 1, 3)
    kpos = jnp.arange(l_len)
    scale = ATT_HEAD_DIM ** -0.5

    def one_block(args):
        i, qi, ci = args
        s = jnp.einsum("bqhd,bkhd->bhqk", qi, k, preferred_element_type=jnp.float32) * scale
        s = s + ci[..., None] - cum_k[:, :, None, :]
        qpos = i * BLOCK_Q + jnp.arange(BLOCK_Q)
        mask = (kpos[None, :] <= qpos[:, None]) & (kpos[None, :] >= pad)
        s = jnp.where(mask[None, None], s, NEG_BIG)
        p = jax.nn.softmax(s, axis=-1)
        return jnp.einsum("bhqk,bkhd->bqhd", p.astype(v.dtype), v)

    out = lax.map(one_block, (jnp.arange(n_blocks), qb, cqb))
    out = out.transpose(1, 0, 2, 3, 4).reshape(bsz, l_len, ATT_DIM)
    return out[:, pad:]


def _cplx_combine(e1, e2):
    a1r, a1i, b1r, b1i = e1
    a2r, a2i, b2r, b2i = e2
    ar = a2r * a1r - a2i * a1i
    ai = a2r * a1i + a2i * a1r
    br = a2r * b1r - a2i * b1i + b2r
    bi = a2r * b1i + a2i * b1r + b2i
    return ar, ai, br, bi


def s5_mixer(u, lam_re, lam_im, log_dt, b_re, b_im, c_re, c_im, d_skip, glu_w, glu_b):
    bsz, t_len = u.shape[0], u.shape[1]
    ug = u.reshape(bsz, t_len, SSM_GROUPS, SSM_GROUP).astype(jnp.float32)
    dt = jnp.exp(log_dt.astype(jnp.float32))[:, None]
    lr, li = lam_re.astype(jnp.float32), lam_im.astype(jnp.float32)
    mag = jnp.exp(lr * dt)
    ab_re, ab_im = mag * jnp.cos(li * dt), mag * jnp.sin(li * dt)
    nr, ni = ab_re - 1.0, ab_im
    den = lr * lr + li * li
    f_re, f_im = (nr * lr + ni * li) / den, (ni * lr - nr * li) / den
    br, bi = b_re.astype(jnp.float32), b_im.astype(jnp.float32)
    bb_re = f_re[..., None] * br - f_im[..., None] * bi
    bb_im = f_re[..., None] * bi + f_im[..., None] * br
    bu_re = jnp.einsum("btgh,gph->tbgp", ug, bb_re)
    bu_im = jnp.einsum("btgh,gph->tbgp", ug, bb_im)
    a_re = jnp.broadcast_to(ab_re[None, None], (t_len, 1, SSM_GROUPS, SSM_STATE))
    a_im = jnp.broadcast_to(ab_im[None, None], (t_len, 1, SSM_GROUPS, SSM_STATE))
    _, _, x_re, x_im = lax.associative_scan(_cplx_combine, (a_re, a_im, bu_re, bu_im), axis=0)
    y = (jnp.einsum("gho,tbgo->btgh", c_re.astype(jnp.float32), x_re)
         - jnp.einsum("gho,tbgo->btgh", c_im.astype(jnp.float32), x_im))
    y = y + d_skip.astype(jnp.float32).reshape(SSM_GROUPS, SSM_GROUP) * ug
    zg = jax.nn.gelu(y)
    gate = jnp.einsum("btgh,ghk->btgk", zg, glu_w.astype(jnp.float32)) + glu_b.astype(jnp.float32)
    out = zg * jax.nn.sigmoid(gate)
    return out.reshape(bsz, t_len, SSM_DIM).astype(u.dtype)


def hier_moe(h, wg, bg, we, be, w1, w3, w2):
    bsz, t_len, d = h.shape
    t = h.reshape(-1, d)
    lg = (t @ wg).astype(jnp.float32) + bg.astype(jnp.float32)
    pg = jax.nn.softmax(lg, axis=-1)
    g_sel = jnp.argmax(lg, axis=-1)
    pg_sel = jnp.take_along_axis(pg, g_sel[:, None], axis=-1)
    le = ((t @ we).astype(jnp.float32) + be.astype(jnp.float32)).reshape(-1, N_EXPERT_GROUPS, EXPERTS_PER_GROUP)
    le_sel = jnp.take_along_axis(le, g_sel[:, None, None], axis=1)[:, 0]
    top_v, top_i = lax.top_k(le_sel, TOP_K_INNER)
    pe = jax.nn.softmax(top_v, axis=-1) * pg_sel
    e_idx = g_sel[:, None] * EXPERTS_PER_GROUP + top_i
    gates = jnp.sum(jax.nn.one_hot(e_idx, N_EXPERTS, dtype=jnp.float32) * pe[..., None], axis=1)
    out = jnp.zeros(t.shape, jnp.float32)
    for e in range(N_EXPERTS):
        y = (jax.nn.silu(t @ w1[e]) * (t @ w3[e])) @ w2[e]
        out = out + gates[:, e:e + 1] * y.astype(jnp.float32)
    return out.reshape(bsz, t_len, d).astype(h.dtype)


def setup_inputs(seed: int = 0) -> dict:
    key = jax.random.key(seed)
    ks = jax.random.split(key, 32)
    nrm = lambda k, shape, s: jax.random.normal(k, shape, jnp.float32) * s
    L, D = DEPTH, D_MODEL
    n_idx = jnp.arange(SSM_STATE, dtype=jnp.float32)
    lam_im = jnp.broadcast_to(math.pi * n_idx, (L, SSM_GROUPS, SSM_STATE)) + nrm(ks[9], (L, SSM_GROUPS, SSM_STATE), 0.01)
    lam_re = -0.5 + nrm(ks[10], (L, SSM_GROUPS, SSM_STATE), 0.01)
    log_dt = jax.random.uniform(ks[11], (L, SSM_GROUPS), jnp.float32, math.log(1e-3), math.log(1e-1))
    return {
        "x": nrm(ks[0], (BATCH, SEQ, D), 1.0),
        "meta_tokens": nrm(ks[1], (N_META, D), 1.0),
        "norm_mix_g": 1.0 + nrm(ks[2], (L, D), 0.02),
        "w_in": nrm(ks[3], (L, D, D_IN), D ** -0.5),
        "fgate_b": 3.0 + nrm(ks[4], (L, ATT_HEADS), 1.0),
        "conv_w": nrm(ks[5], (L, CONV_WIDTH, CONV_DIM), CONV_WIDTH ** -0.5),
        "conv_b": nrm(ks[6], (L, CONV_DIM), 0.02),
        "conv_ln_g": 1.0 + nrm(ks[7], (L, CONV_DIM), 0.02),
        "conv_ln_b": nrm(ks[8], (L, CONV_DIM), 0.02),
        "att_norm_g": 1.0 + nrm(ks[12], (L, ATT_DIM), 0.02),
        "ssm_lam_re": lam_re,
        "ssm_lam_im": lam_im,
        "ssm_log_dt": log_dt,
        "ssm_b_re": nrm(ks[13], (L, SSM_GROUPS, SSM_STATE, SSM_GROUP), (2 * SSM_GROUP) ** -0.5),
        "ssm_b_im": nrm(ks[14], (L, SSM_GROUPS, SSM_STATE, SSM_GROUP), (2 * SSM_GROUP) ** -0.5),
        "ssm_c_re": nrm(ks[15], (L, SSM_GROUPS, SSM_GROUP, SSM_STATE), (2 * SSM_STATE) ** -0.5),
        "ssm_c_im": nrm(ks[16], (L, SSM_GROUPS, SSM_GROUP, SSM_STATE), (2 * SSM_STATE) ** -0.5),
        "ssm_d": nrm(ks[17], (L, SSM_DIM), 1.0),
        "ssm_glu_w": nrm(ks[18], (L, SSM_GROUPS, SSM_GROUP, SSM_GROUP), SSM_GROUP ** -0.5),
        "ssm_glu_b": nrm(ks[19], (L, SSM_GROUPS, SSM_GROUP), 0.02),
        "ssm_norm_g": 1.0 + nrm(ks[20], (L, SSM_DIM), 0.02),
        "w_out": nrm(ks[21], (L, D_MIX, D), D_MIX ** -0.5),
        "norm_ffn_g": 1.0 + nrm(ks[22], (L, D), 0.02),
        "router_g_w": nrm(ks[23], (L, D, N_EXPERT_GROUPS), D ** -0.5),
        "router_g_b": nrm(ks[24], (L, N_EXPERT_GROUPS), 0.01),
        "router_e_w": nrm(ks[25], (L, D, N_EXPERTS), D ** -0.5),
        "router_e_b": nrm(ks[26], (L, N_EXPERTS), 0.01),
        "exp_w1": nrm(ks[27], (L, N_EXPERTS, D, D_FF_EXPERT), D ** -0.5),
        "exp_w3": nrm(ks[28], (L, N_EXPERTS, D, D_FF_EXPERT), D ** -0.5),
        "exp_w2": nrm(ks[29], (L, N_EXPERTS, D_FF_EXPERT, D), D_FF_EXPERT ** -0.5),
        "final_norm_g": 1.0 + nrm(ks[30], (D,), 0.02),
    }


def reference(x, meta_tokens, norm_mix_g, w_in, fgate_b, conv_w, conv_b, conv_ln_g, conv_ln_b,
              att_norm_g, ssm_lam_re, ssm_lam_im, ssm_log_dt, ssm_b_re, ssm_b_im, ssm_c_re,
              ssm_c_im, ssm_d, ssm_glu_w, ssm_glu_b, ssm_norm_g, w_out, norm_ffn_g,
              router_g_w, router_g_b, router_e_w, router_e_b, exp_w1, exp_w3, exp_w2,
              final_norm_g):
    bsz = x.shape[0]
    meta = jnp.broadcast_to(meta_tokens[None].astype(x.dtype), (bsz, N_META, D_MODEL))
    h = jnp.concatenate([meta, x], axis=1)
    t_len = h.shape[1]
    for l in range(DEPTH):
        z = rms_norm(h, norm_mix_g[l])
        proj = z @ w_in[l]
        cv_a, cv_g, q, k, v, f_lg, s_u = jnp.split(proj, IN_SPLITS, axis=-1)
        y_conv = conformer_conv(cv_a, cv_g, conv_w[l], conv_b[l], conv_ln_g[l], conv_ln_b[l])
        hs = (bsz, t_len, ATT_HEADS, ATT_HEAD_DIM)
        y_att = forgetting_attention(q.reshape(hs), k.reshape(hs), v.reshape(hs),
                                     f_lg + fgate_b[l].astype(f_lg.dtype))
        y_ssm = s5_mixer(s_u, ssm_lam_re[l], ssm_lam_im[l], ssm_log_dt[l], ssm_b_re[l], ssm_b_im[l],
                         ssm_c_re[l], ssm_c_im[l], ssm_d[l], ssm_glu_w[l], ssm_glu_b[l])
        mixed = jnp.concatenate([y_conv, rms_norm(y_att, att_norm_g[l]),
                                 rms_norm(y_ssm, ssm_norm_g[l])], axis=-1)
        h = h + mixed @ w_out[l]
        h = h + hier_moe(rms_norm(h, norm_ffn_g[l]), router_g_w[l], router_g_b[l],
                         router_e_w[l], router_e_b[l], exp_w1[l], exp_w3[l], exp_w2[l])
    return rms_norm(h, final_norm_g)[:, N_META:]
```

```python
import functools
import math

import jax
import jax.numpy as jnp
from jax import lax
from jax.experimental import pallas as pl
from jax.experimental.pallas import tpu as pltpu

D_MODEL = 1024
N_META = 16
CONV_DIM = 256
ATT_DIM = 512
HEAD_DIM = 64
N_HEADS = 8
SSM_DIM = 256
SSM_GROUP = 16
SSM_GROUPS = 16
SSM_STATE = 64
CONV_WIDTH = 31
BLOCK_Q = 128
N_GROUPS = 4
EXPERTS_PER_GROUP = 4
N_EXPERTS = 16
D_FF = 256
EPS = 1e-6
PAD = BLOCK_Q - N_META
BIG = 1e30

LANES = 128
CONV_HALO = 32
SSM_CHUNK = 16

_MXU_DTYPE = jnp.bfloat16
_ACT_DTYPE = jnp.bfloat16
_VMEM_LIMIT = 56 * 1024 * 1024


def _dot(a, b):
    return jnp.dot(a, b, preferred_element_type=jnp.float32)


def _dot_nt(a, b):
    return lax.dot_general(a, b, (((1,), (1,)), ((), ())), preferred_element_type=jnp.float32)


def _sigmoid(x):
    return 1.0 / (1.0 + jnp.exp(-x))


def _pick(n, candidates):
    for c in candidates:
        if n % c == 0:
            return c
    return n


def _params(sem):
    return pltpu.CompilerParams(dimension_semantics=sem, vmem_limit_bytes=_VMEM_LIMIT)


def _inproj_kernel(h_ref, g_ref, wc_ref, wqkv_ref, wf_ref, wu_ref,
                   conv_ref, qkv_ref, f_ref, u_ref, *, tm):
    j = pl.program_id(1)
    x = h_ref[0]
    ms = jnp.mean(x * x, axis=-1, keepdims=True)
    z = x * lax.rsqrt(ms + EPS) * g_ref[...]
    row = j * tm + lax.broadcasted_iota(jnp.int32, (tm, 1), 0)
    z = jnp.where(row >= PAD, z, 0.0)
    zb = z.astype(_MXU_DTYPE)
    conv_ref[0] = _dot(zb, wc_ref[...]).astype(conv_ref.dtype)
    qkv_ref[0] = _dot(zb, wqkv_ref[...]).astype(qkv_ref.dtype)
    f_ref[0] = _dot(zb, wf_ref[...])
    u_ref[0] = _dot(zb, wu_ref[...]).astype(u_ref.dtype)


def _inproj(h, g, wc, wqkv, wf, wu):
    bsz, tp, d = h.shape
    tm = _pick(tp, (704, 384, 128))
    full = lambda a: pl.BlockSpec(a.shape, lambda b, j: (0,) * a.ndim)
    row = lambda w: pl.BlockSpec((1, tm, w), lambda b, j: (b, j, 0))
    return pl.pallas_call(
        functools.partial(_inproj_kernel, tm=tm),
        grid=(bsz, tp // tm),
        in_specs=[row(d), full(g), full(wc), full(wqkv), full(wf), full(wu)],
        out_specs=[row(2 * CONV_DIM), row(3 * ATT_DIM), row(LANES), row(SSM_DIM)],
        out_shape=[jax.ShapeDtypeStruct((bsz, tp, 2 * CONV_DIM), _ACT_DTYPE),
                   jax.ShapeDtypeStruct((bsz, tp, 3 * ATT_DIM), _ACT_DTYPE),
                   jax.ShapeDtypeStruct((bsz, tp, LANES), jnp.float32),
                   jax.ShapeDtypeStruct((bsz, tp, SSM_DIM), _ACT_DTYPE)],
        compiler_params=_params(("parallel", "parallel")),
        name="inproj",
    )(h, g, wc, wqkv, wf, wu)


def _conv_kernel(cin_ref, w_ref, b_ref, g_ref, beta_ref, o_ref, scr, *, tp, ch):
    scr[0:CONV_HALO, :] = jnp.zeros((CONV_HALO, CONV_DIM), jnp.float32)
    shift = CONV_HALO - (CONV_WIDTH - 1)

    def body(c, carry):
        base = pl.multiple_of(c * ch, ch)
        blk = cin_ref[0, pl.ds(base, ch), :]
        a = blk[:, :CONV_DIM].astype(jnp.float32)
        gt = blk[:, CONV_DIM:].astype(jnp.float32)
        scr[pl.ds(base + CONV_HALO, ch), :] = a * _sigmoid(gt)
        win = scr[pl.ds(base, ch + CONV_HALO), :]
        acc = jnp.zeros((ch, CONV_DIM), jnp.float32)
        for r in range(8):
            wr = win if r == 0 else pltpu.roll(win, ch + CONV_HALO - r, 0)
            for a in range(CONV_HALO // 8 + 1):
                j = 8 * a + r - shift
                if 0 <= j < CONV_WIDTH:
                    acc = acc + w_ref[j:j + 1, :] * wr[8 * a:8 * a + ch, :]
        y = acc + b_ref[...]
        mu = jnp.mean(y, axis=-1, keepdims=True)
        yc = y - mu
        var = jnp.mean(yc * yc, axis=-1, keepdims=True)
        y = yc * lax.rsqrt(var + EPS) * g_ref[...] + beta_ref[...]
        o_ref[0, pl.ds(base, ch), :] = (y * _sigmoid(y)).astype(o_ref.dtype)
        return carry

    lax.fori_loop(0, tp // ch, body, 0)


def _conv(cin, w, b, g, beta):
    bsz, tp, _ = cin.shape
    ch = _pick(tp, (64,))
    full = lambda a: pl.BlockSpec(a.shape, lambda i: (0,) * a.ndim)
    return pl.pallas_call(
        functools.partial(_conv_kernel, tp=tp, ch=ch),
        grid=(bsz,),
        in_specs=[pl.BlockSpec((1, tp, 2 * CONV_DIM), lambda i: (i, 0, 0)),
                  full(w), full(b), full(g), full(beta)],
        out_specs=pl.BlockSpec((1, tp, CONV_DIM), lambda i: (i, 0, 0)),
        out_shape=jax.ShapeDtypeStruct((bsz, tp, CONV_DIM), _ACT_DTYPE),
        scratch_shapes=[pltpu.VMEM((CONV_HALO + tp, CONV_DIM), jnp.float32)],
        compiler_params=_params(("parallel",)),
        name="conv",
    )(cin, w, b, g, beta)


def _cumsum_kernel(f_ref, b_ref, o_ref, *, tp):
    rows = f_ref.shape[0]
    s_i = lax.broadcasted_iota(jnp.int32, (LANES, LANES), 0)
    t_i = lax.broadcasted_iota(jnp.int32, (LANES, LANES), 1)
    tri = jnp.where(s_i <= t_i, 1.0, 0.0).astype(jnp.bfloat16)
    lane = lax.broadcasted_iota(jnp.int32, (rows, LANES), 1)
    carry = jnp.zeros((rows, 1), jnp.float32)
    for c in range(tp // LANES):
        x = f_ref[:, c * LANES:(c + 1) * LANES] + b_ref[...]
        ls = jnp.minimum(x, 0.0) - jnp.log(1.0 + jnp.exp(-jnp.abs(x)))
        pos = lane + c * LANES
        ls = jnp.where(pos >= PAD, ls, 0.0)
        x1 = ls.astype(jnp.bfloat16)
        r1 = ls - x1.astype(jnp.float32)
        x2 = r1.astype(jnp.bfloat16)
        x3 = (r1 - x2.astype(jnp.float32)).astype(jnp.bfloat16)
        cum = _dot(x1, tri) + _dot(x2, tri) + _dot(x3, tri) + carry
        carry = cum[:, LANES - 1:LANES]
        o_ref[:, c * LANES:(c + 1) * LANES] = jnp.where(pos >= PAD, cum, BIG)


def _cumsum(f, bias):
    rows, tp = f.shape
    return pl.pallas_call(
        functools.partial(_cumsum_kernel, tp=tp),
        out_shape=jax.ShapeDtypeStruct((rows, tp), jnp.float32),
        compiler_params=pltpu.CompilerParams(vmem_limit_bytes=_VMEM_LIMIT),
        name="fgate_cumsum",
    )(f, bias)


def _attn_kernel(q_ref, k_ref, v_ref, ck_ref, o_ref, m_scr, l_scr, acc_scr, *, tq):
    i = pl.program_id(2)
    q2 = q_ref[0]
    lane = lax.broadcasted_iota(jnp.int32, (tq, LANES), 1)
    lo = lane < HEAD_DIM
    zero = jnp.zeros_like(q2)
    qh = (jnp.where(lo, q2, zero), jnp.where(lo, zero, q2))
    m_scr[...] = jnp.full(m_scr.shape, -BIG, jnp.float32)
    l_scr[...] = jnp.zeros(l_scr.shape, jnp.float32)
    acc_scr[...] = jnp.zeros(acc_scr.shape, jnp.float32)
    r_i = lax.broadcasted_iota(jnp.int32, (tq, tq), 0)
    c_i = lax.broadcasted_iota(jnp.int32, (tq, tq), 1)

    def step(j, diagonal):
        start = pl.multiple_of(j * tq, tq)
        kb = k_ref[0, pl.ds(start, tq), :]
        vb = v_ref[0, pl.ds(start, tq), :]
        for h in range(2):
            ck = ck_ref[0, 0, h:h + 1, pl.ds(start, tq)]
            s = _dot_nt(qh[h], kb) - ck
            if diagonal:
                s = jnp.where(c_i <= r_i, s, -BIG)
            m_prev = m_scr[h]
            m_new = jnp.maximum(m_prev, jnp.max(s, axis=-1, keepdims=True))
            alpha = jnp.exp(m_prev - m_new)
            p = jnp.exp(s - m_new[:, :1])
            l_scr[h] = alpha * l_scr[h] + jnp.sum(p, axis=-1, keepdims=True)
            acc_scr[h] = alpha * acc_scr[h] + _dot(p.astype(_MXU_DTYPE), vb)
            m_scr[h] = m_new

    def body(j, carry):
        step(j, False)
        return carry

    lax.fori_loop(0, i, body, 0)
    step(i, True)
    out = jnp.where(lo, acc_scr[0] / l_scr[0], acc_scr[1] / l_scr[1])
    o_ref[0] = out.astype(o_ref.dtype)


def _attention(qkv, ck):
    bsz, tp, _ = qkv.shape
    tq = _pick(tp, (384, 128))
    hp = N_HEADS // 2
    return pl.pallas_call(
        functools.partial(_attn_kernel, tq=tq),
        grid=(bsz, hp, tp // tq),
        in_specs=[pl.BlockSpec((1, tq, LANES), lambda b, p, i: (b, i, p)),
                  pl.BlockSpec((1, tp, LANES), lambda b, p, i: (b, 0, hp + p)),
                  pl.BlockSpec((1, tp, LANES), lambda b, p, i: (b, 0, 2 * hp + p)),
                  pl.BlockSpec((1, 1, 2, tp), lambda b, p, i: (b, p, 0, 0))],
        out_specs=pl.BlockSpec((1, tq, LANES), lambda b, p, i: (b, i, p)),
        out_shape=jax.ShapeDtypeStruct((bsz, tp, ATT_DIM), _ACT_DTYPE),
        scratch_shapes=[pltpu.VMEM((2, tq, LANES), jnp.float32),
                        pltpu.VMEM((2, tq, LANES), jnp.float32),
                        pltpu.VMEM((2, tq, LANES), jnp.float32)],
        compiler_params=_params(("parallel", "parallel", "arbitrary")),
        name="fox_attention",
    )(qkv, qkv, qkv, ck)


def _s5_kernel(u_ref, kbig_ref, fre_ref, fim_ref, ere_ref, eim_ref, are_ref, aim_ref,
               d_ref, gw_ref, gb_ref, o_ref, sre, sim, *, gpb, nchunk, bsz, rb):
    nrows = nchunk * bsz
    for g in range(gpb):
        def fill(r, carry, g=g):
            r0 = pl.multiple_of(r * rb, rb)
            u = u_ref[g, pl.ds(r0, rb), :]
            sre[g, pl.ds(r0, rb), :] = _dot(u, fre_ref[g])
            sim[g, pl.ds(r0, rb), :] = _dot(u, fim_ref[g])
            return carry
        lax.fori_loop(0, nrows // rb, fill, 0)

    def scan(c, carry):
        r0 = pl.multiple_of(c * bsz, bsz)
        new = []
        for g in range(gpb):
            xr, xi = carry[g]
            s_r = sre[g, pl.ds(r0, bsz), :]
            s_i = sim[g, pl.ds(r0, bsz), :]
            sre[g, pl.ds(r0, bsz), :] = xr
            sim[g, pl.ds(r0, bsz), :] = xi
            ar = are_ref[g]
            ai = aim_ref[g]
            new.append((ar * xr - ai * xi + s_r, ar * xi + ai * xr + s_i))
        return tuple(new)

    zero = jnp.zeros((bsz, SSM_STATE), jnp.float32)
    lax.fori_loop(0, nchunk, scan, tuple((zero, zero) for _ in range(gpb)))

    for g in range(gpb):
        def post(r, carry, g=g):
            r0 = pl.multiple_of(r * rb, rb)
            u = u_ref[g, pl.ds(r0, rb), :]
            xr = sre[g, pl.ds(r0, rb), :].astype(_MXU_DTYPE)
            xi = sim[g, pl.ds(r0, rb), :].astype(_MXU_DTYPE)
            y = _dot(u, kbig_ref[g]) + _dot(xr, ere_ref[g]) + _dot(xi, eim_ref[g])
            y = y + d_ref[g] * u.astype(jnp.float32)
            zg = 0.5 * y * (1.0 + jnp.tanh(math.sqrt(2.0 / math.pi) * (y + 0.044715 * (y * y * y))))
            gate = _dot(zg.astype(_MXU_DTYPE), gw_ref[g]) + gb_ref[g]
            o_ref[g, pl.ds(r0, rb), :] = (zg * _sigmoid(gate)).astype(o_ref.dtype)
            return carry
        lax.fori_loop(0, nrows // rb, post, 0)


def _s5(u_rows, prm, bsz):
    ngrp, nrows, width = u_rows.shape
    nchunk = nrows // bsz
    gpb = 4
    rb = _pick(nrows, (192, 128, 64))
    blk = lambda a: pl.BlockSpec((gpb,) + a.shape[1:], lambda i: (i,) + (0,) * (a.ndim - 1))
    ops = (u_rows, prm["kbig"], prm["f_re"], prm["f_im"], prm["e_re"], prm["e_im"],
           prm["al_re"], prm["al_im"], prm["d"], prm["gw"], prm["gb"])
    return pl.pallas_call(
        functools.partial(_s5_kernel, gpb=gpb, nchunk=nchunk, bsz=bsz, rb=rb),
        grid=(ngrp // gpb,),
        in_specs=[blk(a) for a in ops],
        out_specs=blk(u_rows),
        out_shape=jax.ShapeDtypeStruct(u_rows.shape, _ACT_DTYPE),
        scratch_shapes=[pltpu.VMEM((gpb, nrows, SSM_STATE), jnp.float32),
                        pltpu.VMEM((gpb, nrows, SSM_STATE), jnp.float32)],
        compiler_params=_params(("parallel",)),
        name="s5",
    )(*ops)


def _s5_prepare(lam_re, lam_im, log_dt, b_re, b_im, c_re, c_im, d_skip, glu_w, glu_b):
    L, G, H, P = SSM_CHUNK, SSM_GROUPS, SSM_GROUP, SSM_STATE
    hp = lax.Precision.HIGHEST
    f32 = jnp.float32
    dt = jnp.exp(log_dt.astype(f32))[:, None]
    lr, li = lam_re.astype(f32), lam_im.astype(f32)
    steps = jnp.arange(L + 1, dtype=f32)[:, None, None]
    mag = jnp.exp(steps * (lr * dt))
    pw_re = mag * jnp.cos(steps * (li * dt))
    pw_im = mag * jnp.sin(steps * (li * dt))
    nr, ni = pw_re[1] - 1.0, pw_im[1]
    den = lr * lr + li * li
    f_re, f_im = (nr * lr + ni * li) / den, (ni * lr - nr * li) / den
    br, bi = b_re.astype(f32), b_im.astype(f32)
    bb_re = f_re[..., None] * br - f_im[..., None] * bi
    bb_im = f_re[..., None] * bi + f_im[..., None] * br
    cr, ci = c_re.astype(f32), c_im.astype(f32)
    cp_re = cr[None] * pw_re[:, :, None, :] - ci[None] * pw_im[:, :, None, :]
    cp_im = cr[None] * pw_im[:, :, None, :] + ci[None] * pw_re[:, :, None, :]
    kd = (jnp.einsum("dghp,gpk->dghk", cp_re[:L], bb_re, precision=hp)
          - jnp.einsum("dghp,gpk->dghk", cp_im[:L], bb_im, precision=hp))
    jj = jnp.arange(L)[:, None]
    ii = jnp.arange(L)[None, :]
    lag = ii - jj
    kb = jnp.where((lag >= 0)[:, :, None, None, None], kd[jnp.clip(lag, 0, L - 1)], 0.0)
    kbig = kb.transpose(2, 0, 4, 1, 3).reshape(G, L * H, L * H)
    rev_re, rev_im = pw_re[L - 1::-1][:L], pw_im[L - 1::-1][:L]
    fr = rev_re[..., None] * bb_re[None] - rev_im[..., None] * bb_im[None]
    fi = rev_re[..., None] * bb_im[None] + rev_im[..., None] * bb_re[None]
    f_re_m = fr.transpose(1, 0, 3, 2).reshape(G, L * H, P)
    f_im_m = fi.transpose(1, 0, 3, 2).reshape(G, L * H, P)
    e_re = cp_re[1:].transpose(1, 3, 0, 2).reshape(G, P, L * H)
    e_im = (-cp_im[1:]).transpose(1, 3, 0, 2).reshape(G, P, L * H)
    eye = jnp.eye(L, dtype=f32)
    gw = jnp.einsum("ji,ghk->gjhik", eye, glu_w.astype(f32)).reshape(G, L * H, L * H)
    mx = _MXU_DTYPE
    return {
        "kbig": kbig.astype(mx), "f_re": f_re_m.astype(mx), "f_im": f_im_m.astype(mx),
        "e_re": e_re.astype(mx), "e_im": e_im.astype(mx),
        "al_re": pw_re[L][:, None, :], "al_im": pw_im[L][:, None, :],
        "d": jnp.tile(d_skip.astype(f32).reshape(G, 1, H), (1, 1, L)),
        "gw": gw.astype(mx),
        "gb": jnp.tile(glu_b.astype(f32).reshape(G, 1, H), (1, 1, L)),
    }


def _outproj_kernel(h_ref, yc_ref, ya_ref, ys_ref, ga_ref, gs_ref, wc_ref, wa_ref, ws_ref, o_ref):
    ya = ya_ref[...].astype(jnp.float32)
    na = ya * lax.rsqrt(jnp.mean(ya * ya, axis=-1, keepdims=True) + EPS) * ga_ref[...]
    ys = ys_ref[...].astype(jnp.float32)
    ns = ys * lax.rsqrt(jnp.mean(ys * ys, axis=-1, keepdims=True) + EPS) * gs_ref[...]
    acc = _dot(yc_ref[...].astype(_MXU_DTYPE), wc_ref[...])
    acc = acc + _dot(na.astype(_MXU_DTYPE), wa_ref[...])
    acc = acc + _dot(ns.astype(_MXU_DTYPE), ws_ref[...])
    o_ref[...] = h_ref[...] + acc


def _outproj(h, yc, ya, ys, ga, gs, wc, wa, ws):
    n, d = h.shape
    tm = _pick(n, (1024, 768, 128))
    full = lambda a: pl.BlockSpec(a.shape, lambda i: (0,) * a.ndim)
    row = lambda w: pl.BlockSpec((tm, w), lambda i: (i, 0))
    return pl.pallas_call(
        _outproj_kernel,
        grid=(n // tm,),
        in_specs=[row(d), row(CONV_DIM), row(ATT_DIM), row(SSM_DIM),
                  full(ga), full(gs), full(wc), full(wa), full(ws)],
        out_specs=row(d),
        out_shape=jax.ShapeDtypeStruct((n, d), jnp.float32),
        compiler_params=_params(("parallel",)),
        name="outproj",
    )(h, yc, ya, ys, ga, gs, wc, wa, ws)


def _moe_kernel(h_ref, g_ref, wrh_ref, wrl_ref, br_ref, w1_ref, w3_ref, w2_ref, o_ref,
                x_scr, gate_scr, acc_scr, *, tm):
    grp = pl.program_id(1)
    lane = lax.broadcasted_iota(jnp.int32, (tm, LANES), 1)

    @pl.when(grp == 0)
    def _():
        x = h_ref[...]
        z = x * lax.rsqrt(jnp.mean(x * x, axis=-1, keepdims=True) + EPS) * g_ref[...]
        z_hi = z.astype(jnp.bfloat16)
        x_scr[...] = z.astype(x_scr.dtype)
        z_lo = (z - z_hi.astype(jnp.float32)).astype(jnp.bfloat16)
        lg = (_dot(z_hi, wrh_ref[...]) + _dot(z_hi, wrl_ref[...]) + _dot(z_lo, wrh_ref[...])
              + br_ref[...])
        lanef = lane.astype(jnp.float32)
        is_g = lane < N_GROUPS
        lgm = jnp.where(is_g, lg, -BIG)
        mx = jnp.max(lgm, axis=-1, keepdims=True)
        seg = jnp.sum(jnp.where(is_g, jnp.exp(lgm - mx), 0.0), axis=-1, keepdims=True)
        pg_sel = 1.0 / seg
        g_sel = jnp.min(jnp.where(lgm == mx, lanef, float(LANES)), axis=-1, keepdims=True)
        lane_grp = ((lane - N_GROUPS) >> 2).astype(jnp.float32)
        in_sel = jnp.where(lane >= N_GROUPS, lane_grp, -1.0) == g_sel
        v = jnp.where(in_sel, lg, -BIG)
        m1 = jnp.max(v, axis=-1, keepdims=True)
        i1 = jnp.min(jnp.where(v == m1, lanef, float(LANES)), axis=-1, keepdims=True)
        v2 = jnp.where(lanef == i1, -BIG, v)
        m2 = jnp.max(v2, axis=-1, keepdims=True)
        i2 = jnp.min(jnp.where(v2 == m2, lanef, float(LANES)), axis=-1, keepdims=True)
        e2 = jnp.exp(m2 - m1)
        p1 = 1.0 / (1.0 + e2)
        p2 = e2 * p1
        gate_scr[...] = (jnp.where(lanef == i1, p1, 0.0) + jnp.where(lanef == i2, p2, 0.0)) * pg_sel
        acc_scr[...] = jnp.zeros(acc_scr.shape, jnp.float32)

    x = x_scr[...]
    gates = gate_scr[...]
    for k in range(EXPERTS_PER_GROUP):
        h1 = _dot(x, w1_ref[k])
        h3 = _dot(x, w3_ref[k])
        sel = lane == (N_GROUPS + EXPERTS_PER_GROUP * grp + k)
        ge = jnp.sum(jnp.where(sel, gates, 0.0), axis=-1, keepdims=True)
        hh = (h1 * _sigmoid(h1)) * h3 * ge
        acc_scr[...] += _dot(hh.astype(_MXU_DTYPE), w2_ref[k])

    @pl.when(grp == N_GROUPS - 1)
    def _():
        o_ref[...] = h_ref[...] + acc_scr[...]


def _moe(h, g, wr_hi, wr_lo, br, w1, w3, w2):
    n, d = h.shape
    tm = _pick(n, (1024, 768, 128))
    full = lambda a: pl.BlockSpec(a.shape, lambda i, e: (0,) * a.ndim)
    epg = EXPERTS_PER_GROUP
    return pl.pallas_call(
        functools.partial(_moe_kernel, tm=tm),
        grid=(n // tm, N_GROUPS),
        in_specs=[pl.BlockSpec((tm, d), lambda i, e: (i, 0)),
                  full(g), full(wr_hi), full(wr_lo), full(br),
                  pl.BlockSpec((epg, d, D_FF), lambda i, e: (e, 0, 0)),
                  pl.BlockSpec((epg, d, D_FF), lambda i, e: (e, 0, 0)),
                  pl.BlockSpec((epg, D_FF, d), lambda i, e: (e, 0, 0))],
        out_specs=pl.BlockSpec((tm, d), lambda i, e: (i, 0)),
        out_shape=jax.ShapeDtypeStruct((n, d), jnp.float32),
        scratch_shapes=[pltpu.VMEM((tm, d), _MXU_DTYPE),
                        pltpu.VMEM((tm, LANES), jnp.float32),
                        pltpu.VMEM((tm, d), jnp.float32)],
        compiler_params=_params(("parallel", "arbitrary")),
        name="hier_moe",
    )(h, g, wr_hi, wr_lo, br, w1, w3, w2)


def _final_kernel(h_ref, g_ref, o_ref):
    x = h_ref[0]
    o_ref[0] = x * lax.rsqrt(jnp.mean(x * x, axis=-1, keepdims=True) + EPS) * g_ref[...]


def _final_norm(h, g, seq):
    bsz, tp, d = h.shape
    tm = BLOCK_Q
    return pl.pallas_call(
        _final_kernel,
        grid=(bsz, seq // tm),
        in_specs=[pl.BlockSpec((1, tm, d), lambda b, i: (b, i + 1, 0)),
                  pl.BlockSpec(g.shape, lambda b, i: (0, 0))],
        out_specs=pl.BlockSpec((1, tm, d), lambda b, i: (b, i, 0)),
        out_shape=jax.ShapeDtypeStruct((bsz, seq, d), jnp.float32),
        compiler_params=_params(("parallel", "parallel")),
        name="final_norm",
    )(h, g)


def kernel(x, meta_tokens, norm_mix_g, w_in, fgate_b, conv_w, conv_b, conv_ln_g, conv_ln_b, att_norm_g, ssm_lam_re, ssm_lam_im, ssm_log_dt, ssm_b_re, ssm_b_im, ssm_c_re, ssm_c_im, ssm_d, ssm_glu_w, ssm_glu_b, ssm_norm_g, w_out, norm_ffn_g, router_g_w, router_g_b, router_e_w, router_e_b, exp_w1, exp_w3, exp_w2, final_norm_g):
    f32 = jnp.float32
    bsz, seq, d = x.shape
    depth = w_in.shape[0]
    tp = PAD + N_META + seq
    assert tp % LANES == 0 and seq % BLOCK_Q == 0 and tp % SSM_CHUNK == 0
    nchunk = tp // SSM_CHUNK
    row2 = lambda a: a.reshape(1, -1).astype(f32)

    meta = jnp.broadcast_to(meta_tokens[None].astype(f32), (bsz, N_META, d))
    h = jnp.concatenate([jnp.zeros((bsz, PAD, d), f32), meta, x.astype(f32)], axis=1)

    c0 = 2 * CONV_DIM
    c1 = c0 + 3 * ATT_DIM
    c2 = c1 + N_HEADS
    for l in range(depth):
        w = w_in[l].astype(f32)
        wc = w[:, :c0].astype(_MXU_DTYPE)
        wq = w[:, c0:c0 + ATT_DIM] * (HEAD_DIM ** -0.5)
        wqkv = jnp.concatenate([wq, w[:, c0 + ATT_DIM:c1]], axis=1).astype(_MXU_DTYPE)
        wf = jnp.pad(w[:, c1:c2], ((0, 0), (0, LANES - N_HEADS))).astype(_MXU_DTYPE)
        wu = w[:, c2:].astype(_MXU_DTYPE)
        cin, qkv, fl, u = _inproj(h, row2(norm_mix_g[l]), wc, wqkv, wf, wu)

        y_conv = _conv(cin, conv_w[l].astype(f32), row2(conv_b[l]), row2(conv_ln_g[l]), row2(conv_ln_b[l]))

        ft = fl[:, :, :N_HEADS].transpose(0, 2, 1).reshape(bsz * N_HEADS, tp)
        fb = jnp.tile(fgate_b[l].astype(f32), bsz).reshape(bsz * N_HEADS, 1)
        ck = _cumsum(ft, fb).reshape(bsz, N_HEADS // 2, 2, tp)
        y_att = _attention(qkv, ck)

        prm = _s5_prepare(ssm_lam_re[l], ssm_lam_im[l], ssm_log_dt[l], ssm_b_re[l], ssm_b_im[l],
                          ssm_c_re[l], ssm_c_im[l], ssm_d[l], ssm_glu_w[l], ssm_glu_b[l])
        u_rows = (u.reshape(bsz, nchunk, SSM_CHUNK, SSM_GROUPS, SSM_GROUP)
                  .transpose(3, 1, 0, 2, 4).reshape(SSM_GROUPS, nchunk * bsz, SSM_CHUNK * SSM_GROUP))
        y_rows = _s5(u_rows, prm, bsz)
        y_ssm = (y_rows.reshape(SSM_GROUPS, nchunk, bsz, SSM_CHUNK, SSM_GROUP)
                 .transpose(2, 1, 3, 0, 4).reshape(bsz, tp, SSM_DIM))

        n = bsz * tp
        wo = w_out[l].astype(_MXU_DTYPE)
        h2 = _outproj(h.reshape(n, d), y_conv.reshape(n, CONV_DIM), y_att.reshape(n, ATT_DIM),
                      y_ssm.reshape(n, SSM_DIM), row2(att_norm_g[l]), row2(ssm_norm_g[l]),
                      wo[:CONV_DIM], wo[CONV_DIM:CONV_DIM + ATT_DIM], wo[CONV_DIM + ATT_DIM:])

        wr = jnp.concatenate([router_g_w[l], router_e_w[l]], axis=1).astype(f32)
        wr = jnp.pad(wr, ((0, 0), (0, LANES - wr.shape[1])))
        wr_hi = wr.astype(jnp.bfloat16)
        wr_lo = (wr - wr_hi.astype(f32)).astype(jnp.bfloat16)
        br = jnp.concatenate([router_g_b[l], router_e_b[l]]).astype(f32)
        br = jnp.pad(br, (0, LANES - br.shape[0])).reshape(1, LANES)
        h3 = _moe(h2, row2(norm_ffn_g[l]), wr_hi, wr_lo, br,
                  exp_w1[l].astype(_MXU_DTYPE), exp_w3[l].astype(_MXU_DTYPE), exp_w2[l].astype(_MXU_DTYPE))
        h = h3.reshape(bsz, tp, d)

    return _final_norm(h, row2(final_norm_g), seq).astype(x.dtype)
```

```python
import functools
import math

import jax
import jax.numpy as jnp
from jax import lax
from jax.experimental import pallas as pl
from jax.experimental.pallas import tpu as pltpu

D_MODEL = 1024
N_META = 16
CONV_DIM = 256
ATT_DIM = 512
HEAD_DIM = 64
N_HEADS = 8
SSM_DIM = 256
SSM_GROUP = 16
SSM_GROUPS = 16
SSM_STATE = 64
CONV_WIDTH = 31
BLOCK_Q = 128
N_GROUPS = 4
EXPERTS_PER_GROUP = 4
N_EXPERTS = 16
D_FF = 256
EPS = 1e-6
PAD = BLOCK_Q - N_META
BIG = 1e30

LANES = 128
CONV_HALO = 32
SSM_CHUNK = 8
SSM_ROW = SSM_CHUNK * SSM_DIM
SSM_NSTATE = SSM_GROUPS * SSM_STATE

_MXU_DTYPE = jnp.bfloat16
_ACT_DTYPE = jnp.bfloat16
_VMEM_LIMIT = 56 * 1024 * 1024


def _dot(a, b):
    return jnp.dot(a, b, preferred_element_type=jnp.float32)


def _dot_nt(a, b):
    return lax.dot_general(a, b, (((1,), (1,)), ((), ())), preferred_element_type=jnp.float32)


def _sigmoid(x):
    return 1.0 / (1.0 + jnp.exp(-x))


def _rms(x, g):
    return x * lax.rsqrt(jnp.mean(x * x, axis=-1, keepdims=True) + EPS) * g


def _pick(n, candidates):
    for c in candidates:
        if n % c == 0:
            return c
    return n


def _params(sem):
    return pltpu.CompilerParams(dimension_semantics=sem, vmem_limit_bytes=_VMEM_LIMIT)


def _inproj_kernel(h_ref, g_ref, wc_ref, wqkv_ref, wf_ref, wu_ref,
                   conv_ref, qkv_ref, f_ref, u_ref, *, tm):
    j = pl.program_id(1)
    z = _rms(h_ref[0], g_ref[...])
    row = j * tm + lax.broadcasted_iota(jnp.int32, (tm, 1), 0)
    z = jnp.where(row >= PAD, z, 0.0)
    zb = z.astype(_MXU_DTYPE)
    conv_ref[0] = _dot(zb, wc_ref[...]).astype(conv_ref.dtype)
    qkv_ref[0] = _dot(zb, wqkv_ref[...]).astype(qkv_ref.dtype)
    f_ref[0] = _dot(zb, wf_ref[...])
    u_ref[0] = _dot(zb, wu_ref[...]).astype(u_ref.dtype)


def _inproj(h, g, wc, wqkv, wf, wu):
    bsz, tp, d = h.shape
    tm = _pick(tp, (704, 384, 128))
    full = lambda a: pl.BlockSpec(a.shape, lambda b, j: (0,) * a.ndim)
    row = lambda w: pl.BlockSpec((1, tm, w), lambda b, j: (b, j, 0))
    return pl.pallas_call(
        functools.partial(_inproj_kernel, tm=tm),
        grid=(bsz, tp // tm),
        in_specs=[row(d), full(g), full(wc), full(wqkv), full(wf), full(wu)],
        out_specs=[row(2 * CONV_DIM), row(3 * ATT_DIM), row(LANES), row(SSM_DIM)],
        out_shape=[jax.ShapeDtypeStruct((bsz, tp, 2 * CONV_DIM), _ACT_DTYPE),
                   jax.ShapeDtypeStruct((bsz, tp, 3 * ATT_DIM), _ACT_DTYPE),
                   jax.ShapeDtypeStruct((bsz, tp, LANES), jnp.float32),
                   jax.ShapeDtypeStruct((bsz, tp, SSM_DIM), _ACT_DTYPE)],
        compiler_params=_params(("parallel", "parallel")),
        name="inproj",
    )(h, g, wc, wqkv, wf, wu)


def _conv_kernel(cin_ref, w_ref, b_ref, g_ref, beta_ref, o_ref, scr, *, tp, ch):
    scr[0:CONV_HALO, :] = jnp.zeros((CONV_HALO, CONV_DIM), jnp.float32)
    shift = CONV_HALO - (CONV_WIDTH - 1)

    def body(c, carry):
        base = pl.multiple_of(c * ch, ch)
        blk = cin_ref[0, pl.ds(base, ch), :]
        a = blk[:, :CONV_DIM].astype(jnp.float32)
        gt = blk[:, CONV_DIM:].astype(jnp.float32)
        scr[pl.ds(base + CONV_HALO, ch), :] = a * _sigmoid(gt)
        win = scr[pl.ds(base, ch + CONV_HALO), :]
        acc = jnp.zeros((ch, CONV_DIM), jnp.float32)
        for r in range(8):
            wr = win if r == 0 else pltpu.roll(win, ch + CONV_HALO - r, 0)
            for a in range(CONV_HALO // 8 + 1):
                j = 8 * a + r - shift
                if 0 <= j < CONV_WIDTH:
                    acc = acc + w_ref[j:j + 1, :] * wr[8 * a:8 * a + ch, :]
        y = acc + b_ref[...]
        mu = jnp.mean(y, axis=-1, keepdims=True)
        yc = y - mu
        var = jnp.mean(yc * yc, axis=-1, keepdims=True)
        y = yc * lax.rsqrt(var + EPS) * g_ref[...] + beta_ref[...]
        o_ref[0, pl.ds(base, ch), :] = (y * _sigmoid(y)).astype(o_ref.dtype)
        return carry

    lax.fori_loop(0, tp // ch, body, 0)


def _conv(cin, w, b, g, beta):
    bsz, tp, _ = cin.shape
    ch = _pick(tp, (64,))
    full = lambda a: pl.BlockSpec(a.shape, lambda i: (0,) * a.ndim)
    return pl.pallas_call(
        functools.partial(_conv_kernel, tp=tp, ch=ch),
        grid=(bsz,),
        in_specs=[pl.BlockSpec((1, tp, 2 * CONV_DIM), lambda i: (i, 0, 0)),
                  full(w), full(b), full(g), full(beta)],
        out_specs=pl.BlockSpec((1, tp, CONV_DIM), lambda i: (i, 0, 0)),
        out_shape=jax.ShapeDtypeStruct((bsz, tp, CONV_DIM), _ACT_DTYPE),
        scratch_shapes=[pltpu.VMEM((CONV_HALO + tp, CONV_DIM), jnp.float32)],
        compiler_params=_params(("parallel",)),
        name="conv",
    )(cin, w, b, g, beta)


def _cumsum_kernel(f_ref, b_ref, o_ref, *, tp):
    rows = f_ref.shape[0]
    s_i = lax.broadcasted_iota(jnp.int32, (LANES, LANES), 0)
    t_i = lax.broadcasted_iota(jnp.int32, (LANES, LANES), 1)
    tri = jnp.where(s_i <= t_i, 1.0, 0.0).astype(jnp.bfloat16)
    lane = lax.broadcasted_iota(jnp.int32, (rows, LANES), 1)
    carry = jnp.zeros((rows, 1), jnp.float32)
    for c in range(tp // LANES):
        x = f_ref[:, c * LANES:(c + 1) * LANES] + b_ref[...]
        ls = jnp.minimum(x, 0.0) - jnp.log(1.0 + jnp.exp(-jnp.abs(x)))
        pos = lane + c * LANES
        ls = jnp.where(pos >= PAD, ls, 0.0)
        x1 = ls.astype(jnp.bfloat16)
        r1 = ls - x1.astype(jnp.float32)
        x2 = r1.astype(jnp.bfloat16)
        x3 = (r1 - x2.astype(jnp.float32)).astype(jnp.bfloat16)
        cum = _dot(x1, tri) + _dot(x2, tri) + _dot(x3, tri) + carry
        carry = cum[:, LANES - 1:LANES]
        o_ref[:, c * LANES:(c + 1) * LANES] = jnp.where(pos >= PAD, cum, BIG)


def _cumsum(f, bias):
    rows, tp = f.shape
    return pl.pallas_call(
        functools.partial(_cumsum_kernel, tp=tp),
        out_shape=jax.ShapeDtypeStruct((rows, tp), jnp.float32),
        compiler_params=pltpu.CompilerParams(vmem_limit_bytes=_VMEM_LIMIT),
        name="fgate_cumsum",
    )(f, bias)


def _attn_kernel(q_ref, k_ref, v_ref, ck_ref, o_ref, m_scr, l_scr, acc_scr, *, tq, unroll):
    i = pl.program_id(2)
    q2 = q_ref[0]
    lane = lax.broadcasted_iota(jnp.int32, (tq, LANES), 1)
    lo = lane < HEAD_DIM
    zero = jnp.zeros_like(q2)
    qh = (jnp.where(lo, q2, zero), jnp.where(lo, zero, q2))
    m_scr[...] = jnp.full(m_scr.shape, -BIG, jnp.float32)
    l_scr[...] = jnp.zeros(l_scr.shape, jnp.float32)
    acc_scr[...] = jnp.zeros(acc_scr.shape, jnp.float32)
    r_i = lax.broadcasted_iota(jnp.int32, (tq, tq), 0)
    c_i = lax.broadcasted_iota(jnp.int32, (tq, tq), 1)
    nt = tq // LANES

    def scores(j, h, diagonal):
        start = pl.multiple_of(j * tq, tq)
        kb = k_ref[0, pl.ds(start, tq), :]
        ck = ck_ref[0, 0, h:h + 1, pl.ds(start, tq)]
        s = _dot_nt(qh[h], kb) - ck
        if diagonal:
            s = jnp.where(c_i <= r_i, s, -BIG)
        return s

    def pass1(j, diagonal):
        for h in range(2):
            s = scores(j, h, diagonal)
            m = m_scr[h]
            for t in range(nt):
                m = jnp.maximum(m, s[:, t * LANES:(t + 1) * LANES])
            m_scr[h] = m

    def pass2(j, diagonal):
        start = pl.multiple_of(j * tq, tq)
        vb = v_ref[0, pl.ds(start, tq), :]
        for h in range(2):
            s = scores(j, h, diagonal)
            m = m_scr[h]
            l = l_scr[h]
            ps = []
            for t in range(nt):
                p = jnp.exp(s[:, t * LANES:(t + 1) * LANES] - m)
                l = l + p
                ps.append(p.astype(_MXU_DTYPE))
            l_scr[h] = l
            acc_scr[h] += _dot(jnp.concatenate(ps, axis=1), vb)

    def causal_blocks(fn):
        def group(jj, carry):
            for u in range(unroll):
                fn(unroll * jj + u, False)
            return carry
        lax.fori_loop(0, i // unroll, group, 0)
        done = (i // unroll) * unroll

        def tail(r, carry):
            fn(done + r, False)
            return carry
        lax.fori_loop(0, i - done, tail, 0)
        fn(i, True)

    causal_blocks(pass1)
    for h in range(2):
        m_scr[h] = jnp.broadcast_to(jnp.max(m_scr[h], axis=-1, keepdims=True), (tq, LANES))
    causal_blocks(pass2)
    l0 = jnp.sum(l_scr[0], axis=-1, keepdims=True)
    l1 = jnp.sum(l_scr[1], axis=-1, keepdims=True)
    out = jnp.where(lo, acc_scr[0] / l0, acc_scr[1] / l1)
    o_ref[0] = out.astype(o_ref.dtype)


def _attention(qkv, ck):
    bsz, tp, _ = qkv.shape
    tq = _pick(tp, (384, 128))
    hp = N_HEADS // 2
    return pl.pallas_call(
        functools.partial(_attn_kernel, tq=tq, unroll=2),
        grid=(bsz, hp, tp // tq),
        in_specs=[pl.BlockSpec((1, tq, LANES), lambda b, p, i: (b, i, p)),
                  pl.BlockSpec((1, tp, LANES), lambda b, p, i: (b, 0, hp + p)),
                  pl.BlockSpec((1, tp, LANES), lambda b, p, i: (b, 0, 2 * hp + p)),
                  pl.BlockSpec((1, 1, 2, tp), lambda b, p, i: (b, p, 0, 0))],
        out_specs=pl.BlockSpec((1, tq, LANES), lambda b, p, i: (b, i, p)),
        out_shape=jax.ShapeDtypeStruct((bsz, tp, ATT_DIM), _ACT_DTYPE),
        scratch_shapes=[pltpu.VMEM((2, tq, LANES), jnp.float32),
                        pltpu.VMEM((2, tq, LANES), jnp.float32),
                        pltpu.VMEM((2, tq, LANES), jnp.float32)],
        compiler_params=_params(("parallel", "parallel", "arbitrary")),
        name="fox_attention",
    )(qkv, qkv, qkv, ck)


def _s5_kernel(u_ref, kf_ref, ff_ref, ef_ref, ar_ref, ai_ref, d_ref, o_ref, sx, st, *, tcb, bsz):
    ns = SSM_NSTATE

    @pl.when(pl.program_id(0) == 0)
    def _():
        st[...] = jnp.zeros(st.shape, jnp.float32)

    u = u_ref[...]
    sx[...] = _dot(u, ff_ref[...])
    ar = jnp.broadcast_to(ar_ref[...], (bsz, ns))
    ai = jnp.broadcast_to(ai_ref[...], (bsz, ns))

    def scan(c, carry):
        xr, xi = carry
        r0 = pl.multiple_of(c * bsz, bsz)
        s_r = sx[pl.ds(r0, bsz), :ns]
        s_i = sx[pl.ds(r0, bsz), ns:]
        sx[pl.ds(r0, bsz), :ns] = xr
        sx[pl.ds(r0, bsz), ns:] = xi
        return ar * xr - ai * xi + s_r, ar * xi + ai * xr + s_i

    xr, xi = lax.fori_loop(0, tcb, scan, (st[0], st[1]))
    st[0] = xr
    st[1] = xi

    y = _dot(u, kf_ref[...]) + _dot(sx[...].astype(_MXU_DTYPE), ef_ref[...])
    y = y + d_ref[...] * u.astype(jnp.float32)
    zg = 0.5 * y * (1.0 + jnp.tanh(math.sqrt(2.0 / math.pi) * (y + 0.044715 * (y * y * y))))
    o_ref[...] = zg.astype(o_ref.dtype)


def _s5(u_rows, prm, bsz):
    nrows, width = u_rows.shape
    nchunk = nrows // bsz
    tcb = _pick(nchunk, (48,))
    rows = tcb * bsz
    const = lambda a: pl.BlockSpec(a.shape, lambda i: (0,) * a.ndim, pipeline_mode=pl.Buffered(1))
    tile = pl.BlockSpec((rows, width), lambda i: (i, 0))
    ops = (u_rows, prm["kf"], prm["ff"], prm["ef"], prm["ar"], prm["ai"], prm["d"])
    return pl.pallas_call(
        functools.partial(_s5_kernel, tcb=tcb, bsz=bsz),
        grid=(nchunk // tcb,),
        in_specs=[tile] + [const(a) for a in ops[1:]],
        out_specs=tile,
        out_shape=jax.ShapeDtypeStruct(u_rows.shape, _ACT_DTYPE),
        scratch_shapes=[pltpu.VMEM((rows, 2 * SSM_NSTATE), jnp.float32),
                        pltpu.VMEM((2, bsz, SSM_NSTATE), jnp.float32)],
        compiler_params=_params(("arbitrary",)),
        name="s5",
    )(*ops)


def _s5_prepare(lam_re, lam_im, log_dt, b_re, b_im, c_re, c_im, d_skip, glu_w, glu_b):
    L, G, H, P = SSM_CHUNK, SSM_GROUPS, SSM_GROUP, SSM_STATE
    hp = lax.Precision.HIGHEST
    f32 = jnp.float32
    dt = jnp.exp(log_dt.astype(f32))[:, None]
    lr, li = lam_re.astype(f32), lam_im.astype(f32)
    steps = jnp.arange(L + 1, dtype=f32)[:, None, None]
    mag = jnp.exp(steps * (lr * dt))
    pw_re = mag * jnp.cos(steps * (li * dt))
    pw_im = mag * jnp.sin(steps * (li * dt))
    nr, ni = pw_re[1] - 1.0, pw_im[1]
    den = lr * lr + li * li
    f_re, f_im = (nr * lr + ni * li) / den, (ni * lr - nr * li) / den
    br, bi = b_re.astype(f32), b_im.astype(f32)
    bb_re = f_re[..., None] * br - f_im[..., None] * bi
    bb_im = f_re[..., None] * bi + f_im[..., None] * br
    cr, ci = c_re.astype(f32), c_im.astype(f32)
    cp_re = cr[None] * pw_re[:, :, None, :] - ci[None] * pw_im[:, :, None, :]
    cp_im = cr[None] * pw_im[:, :, None, :] + ci[None] * pw_re[:, :, None, :]
    kd = (jnp.einsum("dghp,gpk->dghk", cp_re[:L], bb_re, precision=hp)
          - jnp.einsum("dghp,gpk->dghk", cp_im[:L], bb_im, precision=hp))
    jj = jnp.arange(L)[:, None]
    ii = jnp.arange(L)[None, :]
    lag = ii - jj
    kb = jnp.where((lag >= 0)[:, :, None, None, None], kd[jnp.clip(lag, 0, L - 1)], 0.0)
    eye = jnp.eye(G, dtype=f32)
    mx = _MXU_DTYPE
    kf = jnp.einsum("jighk,gf->jgkifh", kb, eye).reshape(L * G * H, L * G * H).astype(mx)
    rev_re, rev_im = pw_re[L - 1::-1][:L], pw_im[L - 1::-1][:L]
    fr = rev_re[..., None] * bb_re[None] - rev_im[..., None] * bb_im[None]
    fi = rev_re[..., None] * bb_im[None] + rev_im[..., None] * bb_re[None]
    ff = jnp.concatenate(
        [jnp.einsum("jgpk,gf->jgkfp", fr, eye).reshape(L * G * H, G * P),
         jnp.einsum("jgpk,gf->jgkfp", fi, eye).reshape(L * G * H, G * P)], axis=1).astype(mx)
    ef = jnp.concatenate(
        [jnp.einsum("ighp,gf->fpigh", cp_re[1:], eye).reshape(G * P, L * G * H),
         jnp.einsum("ighp,gf->fpigh", -cp_im[1:], eye).reshape(G * P, L * G * H)], axis=0).astype(mx)
    gw = jnp.einsum("ghk,gf->ghfk", glu_w.astype(f32), eye).reshape(G * H, G * H).astype(mx)
    return {
        "kf": kf, "ff": ff, "ef": ef,
        "ar": pw_re[L].reshape(1, G * P), "ai": pw_im[L].reshape(1, G * P),
        "d": jnp.tile(d_skip.astype(f32).reshape(1, G * H), (1, L)),
        "gw": gw, "gb": glu_b.astype(f32).reshape(1, G * H),
    }


def _outproj_kernel(h_ref, yc_ref, ya_ref, zs_ref, gw_ref, gb_ref, ga_ref, gs_ref,
                    wc_ref, wa_ref, ws_ref, o_ref):
    na = _rms(ya_ref[...].astype(jnp.float32), ga_ref[...])
    zs = zs_ref[...]
    gate = _dot(zs.astype(_MXU_DTYPE), gw_ref[...]) + gb_ref[...]
    ns = _rms(zs.astype(jnp.float32) * _sigmoid(gate), gs_ref[...])
    acc = _dot(yc_ref[...].astype(_MXU_DTYPE), wc_ref[...])
    acc = acc + _dot(na.astype(_MXU_DTYPE), wa_ref[...])
    acc = acc + _dot(ns.astype(_MXU_DTYPE), ws_ref[...])
    o_ref[...] = h_ref[...] + acc


def _outproj(h, yc, ya, zs, gw, gb, ga, gs, wc, wa, ws):
    n, d = h.shape
    tm = _pick(n, (1024, 768, 128))
    full = lambda a: pl.BlockSpec(a.shape, lambda i: (0,) * a.ndim)
    row = lambda w: pl.BlockSpec((tm, w), lambda i: (i, 0))
    return pl.pallas_call(
        _outproj_kernel,
        grid=(n // tm,),
        in_specs=[row(d), row(CONV_DIM), row(ATT_DIM), row(SSM_DIM),
                  full(gw), full(gb), full(ga), full(gs), full(wc), full(wa), full(ws)],
        out_specs=row(d),
        out_shape=jax.ShapeDtypeStruct((n, d), jnp.float32),
        compiler_params=_params(("parallel",)),
        name="outproj",
    )(h, yc, ya, zs, gw, gb, ga, gs, wc, wa, ws)


def _moe_kernel(h_ref, g_ref, wrh_ref, wrl_ref, br_ref, w1_ref, w3_ref, w2_ref, gf_ref, o_ref,
                x_scr, gate_scr, acc_scr, *, tm, final):
    grp = pl.program_id(1)
    lane = lax.broadcasted_iota(jnp.int32, (tm, LANES), 1)

    @pl.when(grp == 0)
    def _():
        z = _rms(h_ref[...], g_ref[...])
        z_hi = z.astype(jnp.bfloat16)
        x_scr[...] = z.astype(x_scr.dtype)
        z_lo = (z - z_hi.astype(jnp.float32)).astype(jnp.bfloat16)
        lg = (_dot(z_hi, wrh_ref[...]) + _dot(z_hi, wrl_ref[...]) + _dot(z_lo, wrh_ref[...])
              + br_ref[...])
        lanef = lane.astype(jnp.float32)
        is_g = lane < N_GROUPS
        lgm = jnp.where(is_g, lg, -BIG)
        mx = jnp.max(lgm, axis=-1, keepdims=True)
        seg = jnp.sum(jnp.where(is_g, jnp.exp(lgm - mx), 0.0), axis=-1, keepdims=True)
        pg_sel = 1.0 / seg
        g_sel = jnp.min(jnp.where(lgm == mx, lanef, float(LANES)), axis=-1, keepdims=True)
        lane_grp = ((lane - N_GROUPS) >> 2).astype(jnp.float32)
        in_sel = jnp.where(lane >= N_GROUPS, lane_grp, -1.0) == g_sel
        v = jnp.where(in_sel, lg, -BIG)
        m1 = jnp.max(v, axis=-1, keepdims=True)
        i1 = jnp.min(jnp.where(v == m1, lanef, float(LANES)), axis=-1, keepdims=True)
        v2 = jnp.where(lanef == i1, -BIG, v)
        m2 = jnp.max(v2, axis=-1, keepdims=True)
        i2 = jnp.min(jnp.where(v2 == m2, lanef, float(LANES)), axis=-1, keepdims=True)
        e2 = jnp.exp(m2 - m1)
        p1 = 1.0 / (1.0 + e2)
        p2 = e2 * p1
        gate_scr[...] = (jnp.where(lanef == i1, p1, 0.0) + jnp.where(lanef == i2, p2, 0.0)) * pg_sel
        acc_scr[...] = jnp.zeros(acc_scr.shape, jnp.float32)

    x = x_scr[...]
    gates = gate_scr[...]
    for k in range(EXPERTS_PER_GROUP):
        h1 = _dot(x, w1_ref[k])
        h3 = _dot(x, w3_ref[k])
        sel = lane == (N_GROUPS + EXPERTS_PER_GROUP * grp + k)
        ge = jnp.sum(jnp.where(sel, gates, 0.0), axis=-1, keepdims=True)
        hh = (h1 * _sigmoid(h1)) * h3 * ge
        acc_scr[...] += _dot(hh.astype(_MXU_DTYPE), w2_ref[k])

    @pl.when(grp == N_GROUPS - 1)
    def _():
        y = h_ref[...] + acc_scr[...]
        o_ref[...] = _rms(y, gf_ref[...]) if final else y


def _moe(h, g, wr_hi, wr_lo, br, w1, w3, w2, gf, final):
    n, d = h.shape
    tm = _pick(n, (1024, 768, 128))
    full = lambda a: pl.BlockSpec(a.shape, lambda i, e: (0,) * a.ndim)
    epg = EXPERTS_PER_GROUP
    return pl.pallas_call(
        functools.partial(_moe_kernel, tm=tm, final=final),
        grid=(n // tm, N_GROUPS),
        in_specs=[pl.BlockSpec((tm, d), lambda i, e: (i, 0)),
                  full(g), full(wr_hi), full(wr_lo), full(br),
                  pl.BlockSpec((epg, d, D_FF), lambda i, e: (e, 0, 0)),
                  pl.BlockSpec((epg, d, D_FF), lambda i, e: (e, 0, 0)),
                  pl.BlockSpec((epg, D_FF, d), lambda i, e: (e, 0, 0)),
                  full(gf)],
        out_specs=pl.BlockSpec((tm, d), lambda i, e: (i, 0)),
        out_shape=jax.ShapeDtypeStruct((n, d), jnp.float32),
        scratch_shapes=[pltpu.VMEM((tm, d), _MXU_DTYPE),
                        pltpu.VMEM((tm, LANES), jnp.float32),
                        pltpu.VMEM((tm, d), jnp.float32)],
        compiler_params=_params(("parallel", "arbitrary")),
        name="hier_moe",
    )(h, g, wr_hi, wr_lo, br, w1, w3, w2, gf)


def kernel(x, meta_tokens, norm_mix_g, w_in, fgate_b, conv_w, conv_b, conv_ln_g, conv_ln_b, att_norm_g, ssm_lam_re, ssm_lam_im, ssm_log_dt, ssm_b_re, ssm_b_im, ssm_c_re, ssm_c_im, ssm_d, ssm_glu_w, ssm_glu_b, ssm_norm_g, w_out, norm_ffn_g, router_g_w, router_g_b, router_e_w, router_e_b, exp_w1, exp_w3, exp_w2, final_norm_g):
    f32 = jnp.float32
    bsz, seq, d = x.shape
    depth = w_in.shape[0]
    tp = PAD + N_META + seq
    assert tp % LANES == 0 and tp % SSM_CHUNK == 0
    nchunk = tp // SSM_CHUNK
    n = bsz * tp
    row2 = lambda a: a.reshape(1, -1).astype(f32)

    meta = jnp.broadcast_to(meta_tokens[None].astype(f32), (bsz, N_META, d))
    h = jnp.concatenate([jnp.zeros((bsz, PAD, d), f32), meta, x.astype(f32)], axis=1)

    c0 = 2 * CONV_DIM
    c1 = c0 + 3 * ATT_DIM
    c2 = c1 + N_HEADS
    for l in range(depth):
        w = w_in[l].astype(f32)
        wc = w[:, :c0].astype(_MXU_DTYPE)
        wq = w[:, c0:c0 + ATT_DIM] * (HEAD_DIM ** -0.5)
        wqkv = jnp.concatenate([wq, w[:, c0 + ATT_DIM:c1]], axis=1).astype(_MXU_DTYPE)
        wf = jnp.pad(w[:, c1:c2], ((0, 0), (0, LANES - N_HEADS))).astype(_MXU_DTYPE)
        wu = w[:, c2:].astype(_MXU_DTYPE)
        cin, qkv, fl, u = _inproj(h, row2(norm_mix_g[l]), wc, wqkv, wf, wu)

        y_conv = _conv(cin, conv_w[l].astype(f32), row2(conv_b[l]), row2(conv_ln_g[l]), row2(conv_ln_b[l]))

        ft = fl[:, :, :N_HEADS].transpose(0, 2, 1).reshape(bsz * N_HEADS, tp)
        fb = jnp.tile(fgate_b[l].astype(f32), bsz).reshape(bsz * N_HEADS, 1)
        ck = _cumsum(ft, fb).reshape(bsz, N_HEADS // 2, 2, tp)
        y_att = _attention(qkv, ck)

        prm = _s5_prepare(ssm_lam_re[l], ssm_lam_im[l], ssm_log_dt[l], ssm_b_re[l], ssm_b_im[l],
                          ssm_c_re[l], ssm_c_im[l], ssm_d[l], ssm_glu_w[l], ssm_glu_b[l])
        u_rows = u.reshape(bsz, nchunk, SSM_ROW).transpose(1, 0, 2).reshape(nchunk * bsz, SSM_ROW)
        z_rows = _s5(u_rows, prm, bsz)
        z_ssm = z_rows.reshape(nchunk, bsz, SSM_ROW).transpose(1, 0, 2).reshape(n, SSM_DIM)

        wo = w_out[l].astype(_MXU_DTYPE)
        h2 = _outproj(h.reshape(n, d), y_conv.reshape(n, CONV_DIM), y_att.reshape(n, ATT_DIM), z_ssm,
                      prm["gw"], prm["gb"], row2(att_norm_g[l]), row2(ssm_norm_g[l]),
                      wo[:CONV_DIM], wo[CONV_DIM:CONV_DIM + ATT_DIM], wo[CONV_DIM + ATT_DIM:])

        wr = jnp.concatenate([router_g_w[l], router_e_w[l]], axis=1).astype(f32)
        wr = jnp.pad(wr, ((0, 0), (0, LANES - wr.shape[1])))
        wr_hi = wr.astype(jnp.bfloat16)
        wr_lo = (wr - wr_hi.astype(f32)).astype(jnp.bfloat16)
        br = jnp.concatenate([router_g_b[l], router_e_b[l]]).astype(f32)
        br = jnp.pad(br, (0, LANES - br.shape[0])).reshape(1, LANES)
        h3 = _moe(h2, row2(norm_ffn_g[l]), wr_hi, wr_lo, br,
                  exp_w1[l].astype(_MXU_DTYPE), exp_w3[l].astype(_MXU_DTYPE), exp_w2[l].astype(_MXU_DTYPE),
                  row2(final_norm_g), l == depth - 1)
        h = h3.reshape(bsz, tp, d)

    return h[:, PAD + N_META:, :].astype(x.dtype)
```
